```python
import jax, jax.numpy as jnp
from jax import lax
import numpy as np

D_MODEL = 2048
BATCH = 1
SEQ = 8192
DEPTH = 4

N_MIXERS = 2
POOL_WINDOWS = (2, 4, 8, 16)
N_POOL_GROUPS = 4
POOL_GROUP = D_MODEL // N_POOL_GROUPS
CONV_WIDTH = 31
CONV_PAD = CONV_WIDTH // 2
N_EXPERTS = 32
TOP_K = 4
D_EXPERT = D_MODEL // 2
SWIGLU_LIMIT = 7.0
SWIGLU_ALPHA = 1.702
LN_EPS = 1e-5
DEEPNORM_ALPHA = (2.0 * DEPTH) ** 0.25
DEEPNORM_BETA = (8.0 * DEPTH) ** -0.25
N_POOL_LAYERS = (DEPTH + 1) // 2
N_CONV_LAYERS = DEPTH // 2

kernel_name = "hybrid_pool_conformer_moe_deepnorm"


def layer_norm(x, g, b):
    xf = x.astype(jnp.float32)
    mu = jnp.mean(xf, axis=-1, keepdims=True)
    xc = xf - mu
    var = jnp.mean(xc * xc, axis=-1, keepdims=True)
    return (xc * lax.rsqrt(var + LN_EPS)).astype(x.dtype) * g + b


def pool_mixer(x, w, scale):
    B, S, D = x.shape
    xf = x.astype(jnp.float32).reshape(B, S, N_POOL_GROUPS, POOL_GROUP)
    cs = jnp.concatenate([jnp.zeros((B, 1, N_POOL_GROUPS, POOL_GROUP), jnp.float32),
                          jnp.cumsum(xf, axis=1)], axis=1)
    pos = jnp.arange(S)
    means = []
    for g, win in enumerate(POOL_WINDOWS):
        lo = jnp.clip(pos - win // 2, 0, S)
        hi = jnp.clip(pos + win // 2, 0, S)
        csg = cs[:, :, g]
        wsum = jnp.take(csg, hi, axis=1) - jnp.take(csg, lo, axis=1)
        cnt = (hi - lo).astype(jnp.float32)[None, :, None]
        means.append(wsum / cnt)
    pooled = jnp.stack(means, axis=2) - xf
    y = jnp.einsum('bsgc,gcd->bsgd', pooled.astype(x.dtype), w)
    return y.reshape(B, S, D) * scale


def conv_mixer(x, w1, b1, wdw, bdw, ln_g, ln_b, w2, b2):
    D = x.shape[-1]
    h = x @ w1 + b1
    a, gate = jnp.split(h, 2, axis=-1)
    h = a * jax.nn.sigmoid(gate)
    h = lax.conv_general_dilated(h, wdw[:, None, :], window_strides=(1,),
                                 padding=[(CONV_PAD, CONV_PAD)],
                                 dimension_numbers=('NWC', 'WIO', 'NWC'),
                                 feature_group_count=D) + bdw
    h = jax.nn.silu(layer_norm(h, ln_g, ln_b))
    return h @ w2 + b2


def moe_ffn(x, w_r, b_r, w1, b1, w2, b2):
    B, S, D = x.shape
    T = B * S
    xt = x.reshape(T, D)
    logits = (xt @ w_r + b_r).astype(jnp.float32)
    top_vals, top_idx = lax.top_k(logits, TOP_K)
    gates = jax.nn.softmax(top_vals, axis=-1).astype(x.dtype)
    flat_e = top_idx.reshape(-1)
    order = jnp.argsort(flat_e)
    e_sorted = flat_e[order]
    tok = order // TOP_K
    group_sizes = jnp.bincount(flat_e, length=N_EXPERTS).astype(jnp.int32)
    xs = xt[tok]
    h = lax.ragged_dot(xs, w1, group_sizes) + b1[e_sorted]
    glu = jnp.minimum(h[:, :D_EXPERT], SWIGLU_LIMIT)
    lin = jnp.clip(h[:, D_EXPERT:], -SWIGLU_LIMIT, SWIGLU_LIMIT)
    act = glu * jax.nn.sigmoid(SWIGLU_ALPHA * glu) * (lin + 1.0)
    y = lax.ragged_dot(act, w2, group_sizes) + b2[e_sorted]
    y = y * gates.reshape(-1)[order][:, None]
    out = jax.ops.segment_sum(y, tok, num_segments=T)
    return out.reshape(B, S, D)


def setup_inputs(seed: int = 0) -> dict:
    key = jax.random.key(seed)
    ks = jax.random.split(key, 24)
    D, C, F, E = D_MODEL, POOL_GROUP, D_EXPERT, N_EXPERTS
    nrm = jax.random.normal
    f32 = jnp.float32
    return {
        "x": nrm(ks[0], (BATCH, SEQ, D), f32),
        "pool_w": nrm(ks[1], (N_POOL_LAYERS, N_POOL_GROUPS, C, C), f32) * (C ** -0.5) * DEEPNORM_BETA,
        "pool_scale": 1.0 + 0.02 * nrm(ks[2], (N_POOL_LAYERS, D), f32),
        "conv_w1": nrm(ks[3], (N_CONV_LAYERS, D, 2 * D), f32) * (D ** -0.5),
        "conv_b1": 0.02 * nrm(ks[4], (N_CONV_LAYERS, 2 * D), f32),
        "conv_wdw": nrm(ks[5], (N_CONV_LAYERS, CONV_WIDTH, D), f32) * (CONV_WIDTH ** -0.5),
        "conv_bdw": 0.02 * nrm(ks[6], (N_CONV_LAYERS, D), f32),
        "conv_ln_g": 1.0 + 0.02 * nrm(ks[7], (N_CONV_LAYERS, D), f32),
        "conv_ln_b": 0.02 * nrm(ks[8], (N_CONV_LAYERS, D), f32),
        "conv_w2": nrm(ks[9], (N_CONV_LAYERS, D, D), f32) * (D ** -0.5) * DEEPNORM_BETA,
        "conv_b2": 0.02 * nrm(ks[10], (N_CONV_LAYERS, D), f32),
        "mix_ln_g": 1.0 + 0.02 * nrm(ks[11], (DEPTH, D), f32),
        "mix_ln_b": 0.02 * nrm(ks[12], (DEPTH, D), f32),
        "router_w": nrm(ks[13], (DEPTH, D, E), f32) * (D ** -0.5),
        "router_b": 0.01 * nrm(ks[14], (DEPTH, E), f32),
        "moe_w1": nrm(ks[15], (DEPTH, E, D, 2 * F), f32) * (D ** -0.5),
        "moe_b1": 0.02 * nrm(ks[16], (DEPTH, E, 2 * F), f32),
        "moe_w2": nrm(ks[17], (DEPTH, E, F, D), f32) * (F ** -0.5) * DEEPNORM_BETA,
        "moe_b2": 0.02 * nrm(ks[18], (DEPTH, E, D), f32),
        "ffn_ln_g": 1.0 + 0.02 * nrm(ks[19], (DEPTH, D), f32),
        "ffn_ln_b": 0.02 * nrm(ks[20], (DEPTH, D), f32),
    }


def reference(x, pool_w, pool_scale, conv_w1, conv_b1, conv_wdw, conv_bdw, conv_ln_g,
              conv_ln_b, conv_w2, conv_b2, mix_ln_g, mix_ln_b, router_w, router_b,
              moe_w1, moe_b1, moe_w2, moe_b2, ffn_ln_g, ffn_ln_b):
    for i in range(DEPTH):
        j = i // N_MIXERS
        if i % N_MIXERS == 0:
            mix = pool_mixer(x, pool_w[j], pool_scale[j])
        else:
            mix = conv_mixer(x, conv_w1[j], conv_b1[j], conv_wdw[j], conv_bdw[j],
                             conv_ln_g[j], conv_ln_b[j], conv_w2[j], conv_b2[j])
        x = layer_norm(DEEPNORM_ALPHA * x + mix, mix_ln_g[i], mix_ln_b[i])
        ffn = moe_ffn(x, router_w[i], router_b[i], moe_w1[i], moe_b1[i], moe_w2[i], moe_b2[i])
        x = layer_norm(DEEPNORM_ALPHA * x + ffn, ffn_ln_g[i], ffn_ln_b[i])
    return x
```

```python
import functools

import jax
import jax.numpy as jnp
from jax import lax
from jax.experimental import pallas as pl
from jax.experimental.pallas import tpu as pltpu

POOL_WINDOWS = (2, 4, 8, 16)
TOP_K = 4
SWIGLU_LIMIT = 7.0
SWIGLU_ALPHA = 1.702
LN_EPS = 1e-5

HALO = 16
MIX_TILE = 256
GLU_ROW_TILE = 512
GLU_COL_TILE = 512
MOE_TILE = 256
VMEM_LIMIT_BYTES = 56 * 1024 * 1024

F32 = jnp.float32
BF16 = jnp.bfloat16


def _layer_norm(z, g, b):
    mu = jnp.mean(z, axis=-1, keepdims=True)
    zc = z - mu
    var = jnp.mean(zc * zc, axis=-1, keepdims=True)
    return zc * lax.rsqrt(var + LN_EPS) * g + b


def _sigmoid(v):
    return 1.0 / (1.0 + jnp.exp(-v))


def _params(*sem):
    return pltpu.CompilerParams(dimension_semantics=sem, vmem_limit_bytes=VMEM_LIMIT_BYTES)


def _halo_specs(tm, seq, d):
    r = tm // HALO
    last = seq // HALO - 1
    return [
        pl.BlockSpec((HALO, d), lambda i: (jnp.maximum(i * r - 1, 0), 0)),
        pl.BlockSpec((tm, d), lambda i: (i, 0)),
        pl.BlockSpec((HALO, d), lambda i: (jnp.minimum((i + 1) * r, last), 0)),
    ]


def _fill_halo_buffer(buf_ref, prev_ref, cur, next_ref, i, n_tiles, tm):
    buf_ref[0:HALO, :] = jnp.where(i > 0, prev_ref[...], 0.0)
    buf_ref[HALO:HALO + tm, :] = cur
    buf_ref[HALO + tm:HALO + tm + HALO, :] = jnp.where(i < n_tiles - 1, next_ref[...], 0.0)


def _pool_ln_kernel(prev_ref, cur_ref, next_ref, w_ref, scale_ref, g_ref, b_ref, out_ref, buf_ref,
                    *, tm, n_tiles, seq, alpha):
    i = pl.program_id(0)
    d = cur_ref.shape[1]
    c = d // len(POOL_WINDOWS)
    x = cur_ref[...]
    _fill_halo_buffer(buf_ref, prev_ref, x, next_ref, i, n_tiles, tm)
    t = i * tm + lax.broadcasted_iota(jnp.int32, (tm, 1), 0)
    ys = []
    for g, win in enumerate(POOL_WINDOWS):
        h = win // 2
        cols = slice(g * c, (g + 1) * c)
        wsum = buf_ref[HALO - h:HALO - h + tm, cols]
        for j in range(1, win):
            wsum = wsum + buf_ref[HALO - h + j:HALO - h + j + tm, cols]
        cnt = (jnp.minimum(t + h, seq) - jnp.maximum(t - h, 0)).astype(F32)
        pooled = wsum / cnt - x[:, cols]
        ys.append(jnp.dot(pooled.astype(BF16), w_ref[g], preferred_element_type=F32))
    y = jnp.concatenate(ys, axis=1) * scale_ref[...]
    out_ref[...] = _layer_norm(alpha * x + y, g_ref[...], b_ref[...])


def _pool_ln(x, w_bf16, scale, ln_g, ln_b, alpha):
    seq, d = x.shape
    tm = min(MIX_TILE, seq)
    n_tiles = seq // tm
    g, c, _ = w_bf16.shape
    row = lambda: pl.BlockSpec((1, d), lambda i: (0, 0))
    return pl.pallas_call(
        functools.partial(_pool_ln_kernel, tm=tm, n_tiles=n_tiles, seq=seq, alpha=alpha),
        grid=(n_tiles,),
        in_specs=_halo_specs(tm, seq, d) + [pl.BlockSpec((g, c, c), lambda i: (0, 0, 0)), row(), row(), row()],
        out_specs=pl.BlockSpec((tm, d), lambda i: (i, 0)),
        out_shape=jax.ShapeDtypeStruct((seq, d), F32),
        scratch_shapes=[pltpu.VMEM((tm + 2 * HALO, d), F32)],
        compiler_params=_params("arbitrary"),
        name="pool_ln",
    )(x, x, x, w_bf16, scale, ln_g, ln_b)


def _glu_kernel(x_ref, wa_ref, wg_ref, ba_ref, bg_ref, out_ref):
    x = x_ref[...].astype(BF16)
    a = jnp.dot(x, wa_ref[...], preferred_element_type=F32) + ba_ref[...]
    gate = jnp.dot(x, wg_ref[...], preferred_element_type=F32) + bg_ref[...]
    out_ref[...] = a * _sigmoid(gate)


def _glu(x, w1_bf16, b1):
    seq, d = x.shape
    tm = min(GLU_ROW_TILE, seq)
    tn = min(GLU_COL_TILE, d)
    nj = d // tn
    return pl.pallas_call(
        _glu_kernel,
        grid=(nj, seq // tm),
        in_specs=[
            pl.BlockSpec((tm, d), lambda j, i: (i, 0)),
            pl.BlockSpec((d, tn), lambda j, i: (0, j)),
            pl.BlockSpec((d, tn), lambda j, i: (0, nj + j)),
            pl.BlockSpec((1, tn), lambda j, i: (0, j)),
            pl.BlockSpec((1, tn), lambda j, i: (0, nj + j)),
        ],
        out_specs=pl.BlockSpec((tm, tn), lambda j, i: (i, j)),
        out_shape=jax.ShapeDtypeStruct((seq, d), F32),
        compiler_params=_params("arbitrary", "arbitrary"),
        name="conv_glu",
    )(x, w1_bf16, w1_bf16, b1, b1)


def _conv_ln_kernel(hprev_ref, hcur_ref, hnext_ref, x_ref, wdw_ref, bdw_ref, cg_ref, cb_ref, w2_ref, b2_ref,
                    mg_ref, mb_ref, out_ref, buf_ref, conv_ref, *, tm, n_tiles, alpha, width):
    i = pl.program_id(0)
    d = x_ref.shape[1]
    pad = width // 2
    _fill_halo_buffer(buf_ref, hprev_ref, hcur_ref[...], hnext_ref, i, n_tiles, tm)
    rows = min(64, tm)
    for cb in range(d // 128):
        cols = slice(cb * 128, (cb + 1) * 128)
        taps = [wdw_ref[j:j + 1, cols] for j in range(width)]
        bias = bdw_ref[:, cols]
        for rb in range(tm // rows):
            r0 = HALO - pad + rb * rows
            acc = bias + buf_ref[r0:r0 + rows, cols] * taps[0]
            for j in range(1, width):
                acc = acc + buf_ref[r0 + j:r0 + j + rows, cols] * taps[j]
            conv_ref[rb * rows:(rb + 1) * rows, cols] = acc
    hn = _layer_norm(conv_ref[...], cg_ref[...], cb_ref[...])
    hs = hn * _sigmoid(hn)
    y = jnp.dot(hs.astype(BF16), w2_ref[...], preferred_element_type=F32) + b2_ref[...]
    out_ref[...] = _layer_norm(alpha * x_ref[...] + y, mg_ref[...], mb_ref[...])


def _conv_ln(h, x, wdw, bdw, cg, cb, w2_bf16, b2, mg, mb, alpha):
    seq, d = x.shape
    tm = min(MIX_TILE, seq)
    n_tiles = seq // tm
    width = wdw.shape[0]
    row = lambda: pl.BlockSpec((1, d), lambda i: (0, 0))
    return pl.pallas_call(
        functools.partial(_conv_ln_kernel, tm=tm, n_tiles=n_tiles, alpha=alpha, width=width),
        grid=(n_tiles,),
        in_specs=_halo_specs(tm, seq, d) + [
            pl.BlockSpec((tm, d), lambda i: (i, 0)),
            pl.BlockSpec((width, d), lambda i: (0, 0)),
            row(), row(), row(),
            pl.BlockSpec((d, d), lambda i: (0, 0)),
            row(), row(), row(),
        ],
        out_specs=pl.BlockSpec((tm, d), lambda i: (i, 0)),
        out_shape=jax.ShapeDtypeStruct((seq, d), F32),
        scratch_shapes=[pltpu.VMEM((tm + 2 * HALO, d), F32), pltpu.VMEM((tm, d), F32)],
        compiler_params=_params("arbitrary"),
        name="conv_ln",
    )(h, h, h, x, wdw, bdw, cg, cb, w2_bf16, b2, mg, mb)


def _router_kernel(x_ref, wr_ref, br_ref, idx_ref, gate_ref, rank_ref, cnt_ref, carry_ref, *, tm, n_exp):
    i = pl.program_id(0)

    @pl.when(i == 0)
    def _():
        carry_ref[...] = jnp.zeros_like(carry_ref)

    logits = jnp.dot(x_ref[...], wr_ref[...], precision=lax.Precision.HIGHEST,
                     preferred_element_type=F32) + br_ref[...]
    lane = lax.broadcasted_iota(jnp.int32, (tm, n_exp), 1)
    work = logits
    vals, sels, hots = [], [], []
    for _ in range(TOP_K):
        m = jnp.max(work, axis=-1, keepdims=True)
        sel = jnp.min(jnp.where(work == m, lane, n_exp), axis=-1, keepdims=True)
        hot = lane == sel
        work = jnp.where(hot, -jnp.inf, work)
        vals.append(m)
        sels.append(sel)
        hots.append(hot)
    exps = [jnp.exp(v - vals[0]) for v in vals]
    denom = functools.reduce(lambda a, b: a + b, exps)
    multi = functools.reduce(lambda a, b: a + b, [h.astype(F32) for h in hots])
    strict_lower = (lax.broadcasted_iota(jnp.int32, (tm, tm), 0) > lax.broadcasted_iota(jnp.int32, (tm, tm), 1))
    cum = jnp.dot(strict_lower.astype(BF16), multi.astype(BF16), preferred_element_type=F32) + carry_ref[...]
    slot = lax.broadcasted_iota(jnp.int32, (tm, TOP_K), 1)
    idx = jnp.zeros((tm, TOP_K), jnp.int32)
    gate = jnp.zeros((tm, TOP_K), F32)
    rank = jnp.zeros((tm, TOP_K), jnp.int32)
    for k in range(TOP_K):
        r = jnp.sum(jnp.where(hots[k], cum, 0.0), axis=-1, keepdims=True)
        idx = jnp.where(slot == k, sels[k], idx)
        gate = jnp.where(slot == k, exps[k] / denom, gate)
        rank = jnp.where(slot == k, r.astype(jnp.int32), rank)
    idx_ref[...] = idx
    gate_ref[...] = gate
    rank_ref[...] = rank
    carry_ref[...] += jnp.sum(multi, axis=0, keepdims=True)
    cnt_ref[...] = carry_ref[...].astype(jnp.int32)


def _router(x, w_r, b_r):
    seq, d = x.shape
    n_exp = w_r.shape[1]
    tm = min(MIX_TILE, seq)
    per_tok = lambda dt: jax.ShapeDtypeStruct((seq, TOP_K), dt)
    tok_spec = lambda: pl.BlockSpec((tm, TOP_K), lambda i: (i, 0))
    return pl.pallas_call(
        functools.partial(_router_kernel, tm=tm, n_exp=n_exp),
        grid=(seq // tm,),
        in_specs=[
            pl.BlockSpec((tm, d), lambda i: (i, 0)),
            pl.BlockSpec((d, n_exp), lambda i: (0, 0)),
            pl.BlockSpec((1, n_exp), lambda i: (0, 0)),
        ],
        out_specs=[tok_spec(), tok_spec(), tok_spec(), pl.BlockSpec((1, n_exp), lambda i: (0, 0))],
        out_shape=[per_tok(jnp.int32), per_tok(F32), per_tok(jnp.int32), jax.ShapeDtypeStruct((1, n_exp), jnp.int32)],
        scratch_shapes=[pltpu.VMEM((1, n_exp), F32)],
        compiler_params=_params("arbitrary"),
        name="router",
    )(x, w_r, b_r)


def _moe_kernel(te_ref, nused_ref, dst_ref, x_hbm, w1_ref, b1_ref, w2_ref, b2_ref, y_hbm,
                xbuf, ybuf, gsem, ssem, *, tm, seq, n_real, d_ff):
    i = pl.program_id(0)
    nused = nused_ref[0]
    slot = i % 2

    def gather_row(p, j, s):
        dst = dst_ref[p]
        tok = jnp.where(dst < n_real, lax.rem(dst, seq), 0)
        return pltpu.make_async_copy(x_hbm.at[pl.ds(tok, 1)], xbuf.at[s, pl.ds(j, 1)], gsem.at[s])

    def scatter_row(p, j, s):
        return pltpu.make_async_copy(ybuf.at[s, pl.ds(j, 1)], y_hbm.at[pl.ds(dst_ref[p], 1)], ssem.at[s])

    def start_rows(make, tile, s):
        def body(j, carry):
            make(tile * tm + j, j, s).start()
            return carry
        lax.fori_loop(0, tm, body, 0)

    def wait_rows(make, s):
        def body(j, carry):
            make(0, j, s).wait()
            return carry
        lax.fori_loop(0, tm, body, 0)

    @pl.when(i == 0)
    def _():
        start_rows(gather_row, 0, 0)
        ybuf[1] = jnp.zeros((tm, ybuf.shape[2]), F32)
        zero_dump = pltpu.make_async_copy(ybuf.at[1], y_hbm.at[pl.ds(n_real, tm)], ssem.at[1])
        zero_dump.start()
        zero_dump.wait()

    @pl.when(i < nused)
    def _():
        wait_rows(gather_row, slot)

        @pl.when(i + 1 < nused)
        def _():
            start_rows(gather_row, i + 1, 1 - slot)

        @pl.when(i >= 2)
        def _():
            wait_rows(scatter_row, slot)

        x = xbuf[slot].astype(BF16)
        h = jnp.dot(x, w1_ref[0], preferred_element_type=F32) + b1_ref[0]
        glu = jnp.minimum(h[:, :d_ff], SWIGLU_LIMIT)
        lin = jnp.clip(h[:, d_ff:], -SWIGLU_LIMIT, SWIGLU_LIMIT)
        act = glu * _sigmoid(SWIGLU_ALPHA * glu) * (lin + 1.0)
        ybuf[slot] = jnp.dot(act.astype(BF16), w2_ref[0], preferred_element_type=F32) + b2_ref[0]
        start_rows(scatter_row, i, slot)

        @pl.when(i == nused - 1)
        def _():
            wait_rows(scatter_row, slot)

            @pl.when(i >= 1)
            def _():
                wait_rows(scatter_row, 1 - slot)


def _moe(x, tile_expert, n_used, dst, w1_bf16, b1, w2_bf16, b2, tm):
    seq, d = x.shape
    n_exp, _, two_f = w1_bf16.shape
    d_ff = two_f // 2
    n_real = seq * TOP_K
    n_tiles = dst.shape[0] // tm
    grid_spec = pltpu.PrefetchScalarGridSpec(
        num_scalar_prefetch=3,
        grid=(n_tiles,),
        in_specs=[
            pl.BlockSpec(memory_space=pl.ANY),
            pl.BlockSpec((1, d, two_f), lambda i, te, nu, ds: (te[i], 0, 0)),
            pl.BlockSpec((1, 1, two_f), lambda i, te, nu, ds: (te[i], 0, 0)),
            pl.BlockSpec((1, d_ff, d), lambda i, te, nu, ds: (te[i], 0, 0)),
            pl.BlockSpec((1, 1, d), lambda i, te, nu, ds: (te[i], 0, 0)),
        ],
        out_specs=pl.BlockSpec(memory_space=pl.ANY),
        scratch_shapes=[
            pltpu.VMEM((2, tm, d), F32),
            pltpu.VMEM((2, tm, d), F32),
            pltpu.SemaphoreType.DMA((2,)),
            pltpu.SemaphoreType.DMA((2,)),
        ],
    )
    return pl.pallas_call(
        functools.partial(_moe_kernel, tm=tm, seq=seq, n_real=n_real, d_ff=d_ff),
        grid_spec=grid_spec,
        out_shape=jax.ShapeDtypeStruct((n_real + tm, d), F32),
        compiler_params=_params("arbitrary"),
        name="moe_ffn",
    )(tile_expert, n_used, dst, x, w1_bf16, b1.reshape(n_exp, 1, two_f), w2_bf16, b2.reshape(n_exp, 1, d))


def _route_plan(idx, rank, counts, tm, n_pad):
    seq, k = idx.shape
    n_exp = counts.shape[0]
    padded = ((counts + tm - 1) // tm) * tm
    gend = jnp.cumsum(padded)
    gstart = gend - padded
    pos = gstart[idx] + rank
    out_row = jnp.arange(k, dtype=jnp.int32)[None, :] * seq + jnp.arange(seq, dtype=jnp.int32)[:, None]
    dump = seq * k + jnp.arange(n_pad, dtype=jnp.int32) % tm
    dst = jnp.full((n_pad,), -1, jnp.int32).at[pos.reshape(-1)].set(out_row.reshape(-1))
    dst = jnp.where(dst < 0, dump, dst)
    n_used = (gend[-1] // tm).astype(jnp.int32)
    tile_start = jnp.arange(n_pad // tm, dtype=jnp.int32) * tm
    tile_start = jnp.minimum(tile_start, gend[-1] - tm)
    tile_expert = jnp.sum(tile_start[:, None] >= gend[None, :], axis=1).astype(jnp.int32)
    tile_expert = jnp.minimum(tile_expert, n_exp - 1)
    return tile_expert, n_used.reshape(1), dst


def _combine_ln_kernel(*refs, alpha):
    y_refs = refs[:TOP_K]
    gate_ref, x_ref, g_ref, b_ref, out_ref = refs[TOP_K:]
    gate = gate_ref[...]
    ffn = y_refs[0][...] * gate[:, 0:1]
    for k in range(1, TOP_K):
        ffn = ffn + y_refs[k][...] * gate[:, k:k + 1]
    out_ref[...] = _layer_norm(alpha * x_ref[...] + ffn, g_ref[...], b_ref[...])


def _combine_ln(y, gates, x, ln_g, ln_b, alpha):
    seq, d = x.shape
    tm = min(MIX_TILE, seq)
    n_tiles = seq // tm
    row = lambda: pl.BlockSpec((1, d), lambda i: (0, 0))
    y_specs = [pl.BlockSpec((tm, d), functools.partial(lambda i, k: (k * n_tiles + i, 0), k=k)) for k in range(TOP_K)]
    return pl.pallas_call(
        functools.partial(_combine_ln_kernel, alpha=alpha),
        grid=(n_tiles,),
        in_specs=y_specs + [pl.BlockSpec((tm, TOP_K), lambda i: (i, 0)), pl.BlockSpec((tm, d), lambda i: (i, 0)),
                            row(), row()],
        out_specs=pl.BlockSpec((tm, d), lambda i: (i, 0)),
        out_shape=jax.ShapeDtypeStruct((seq, d), F32),
        compiler_params=_params("arbitrary"),
        name="combine_ln",
    )(*([y] * TOP_K), gates, x, ln_g, ln_b)


def kernel(x, pool_w, pool_scale, conv_w1, conv_b1, conv_wdw, conv_bdw, conv_ln_g, conv_ln_b, conv_w2, conv_b2,
           mix_ln_g, mix_ln_b, router_w, router_b, moe_w1, moe_b1, moe_w2, moe_b2, ffn_ln_g, ffn_ln_b):
    batch, seq, d = x.shape
    depth = mix_ln_g.shape[0]
    n_exp = router_w.shape[-1]
    alpha = (2.0 * depth) ** 0.25
    tm_moe = min(MOE_TILE, seq)
    n_pad = seq * TOP_K + n_exp * tm_moe
    row = lambda v: v.reshape(1, -1)
    outs = []
    for b in range(batch):
        xb = x[b]
        for i in range(depth):
            j = i // 2
            if i % 2 == 0:
                xb = _pool_ln(xb, pool_w[j].astype(BF16), row(pool_scale[j]), row(mix_ln_g[i]), row(mix_ln_b[i]), alpha)
            else:
                h = _glu(xb, conv_w1[j].astype(BF16), row(conv_b1[j]))
                xb = _conv_ln(h, xb, conv_wdw[j], row(conv_bdw[j]), row(conv_ln_g[j]), row(conv_ln_b[j]),
                              conv_w2[j].astype(BF16), row(conv_b2[j]), row(mix_ln_g[i]), row(mix_ln_b[i]), alpha)
            idx, gates, rank, counts = _router(xb, router_w[i], row(router_b[i]))
            tile_expert, n_used, dst = _route_plan(idx, rank, counts[0], tm_moe, n_pad)
            y = _moe(xb, tile_expert, n_used, dst, moe_w1[i].astype(BF16), moe_b1[i], moe_w2[i].astype(BF16),
                     moe_b2[i], tm_moe)
            xb = _combine_ln(y, gates, xb, row(ffn_ln_g[i]), row(ffn_ln_b[i]), alpha)
        outs.append(xb)
    return jnp.stack(outs, axis=0)
```

```python
import functools

import jax
import jax.numpy as jnp
from jax import lax
from jax.experimental import pallas as pl
from jax.experimental.pallas import tpu as pltpu

POOL_WINDOWS = (2, 4, 8, 16)
TOP_K = 4
SWIGLU_LIMIT = 7.0
SWIGLU_ALPHA = 1.702
LN_EPS = 1e-5

HALO = 16
MIX_TILE = 256
GLU_ROW_TILE = 512
GLU_COL_TILE = 512
MOE_TILE = 256
VMEM_LIMIT_BYTES = 56 * 1024 * 1024

F32 = jnp.float32
BF16 = jnp.bfloat16


def _layer_norm(z, g, b):
    mu = jnp.mean(z, axis=-1, keepdims=True)
    zc = z - mu
    var = jnp.mean(zc * zc, axis=-1, keepdims=True)
    return zc * lax.rsqrt(var + LN_EPS) * g + b


def _sigmoid(v):
    return 1.0 / (1.0 + jnp.exp(-v))


def _params(*sem):
    return pltpu.CompilerParams(dimension_semantics=sem, vmem_limit_bytes=VMEM_LIMIT_BYTES)


def _halo_specs(tm, seq, d):
    r = tm // HALO
    last = seq // HALO - 1
    return [
        pl.BlockSpec((HALO, d), lambda i: (jnp.maximum(i * r - 1, 0), 0)),
        pl.BlockSpec((tm, d), lambda i: (i, 0)),
        pl.BlockSpec((HALO, d), lambda i: (jnp.minimum((i + 1) * r, last), 0)),
    ]


def _fill_halo_buffer(buf_ref, prev_ref, cur, next_ref, i, n_tiles, tm):
    buf_ref[0:HALO, :] = jnp.where(i > 0, prev_ref[...], 0.0)
    buf_ref[HALO:HALO + tm, :] = cur
    buf_ref[HALO + tm:HALO + tm + HALO, :] = jnp.where(i < n_tiles - 1, next_ref[...], 0.0)


def _pool_ln_kernel(prev_ref, cur_ref, next_ref, w_ref, scale_ref, g_ref, b_ref, out_ref, buf_ref,
                    *, tm, n_tiles, seq, alpha):
    i = pl.program_id(0)
    d = cur_ref.shape[1]
    c = d // len(POOL_WINDOWS)
    x = cur_ref[...]
    _fill_halo_buffer(buf_ref, prev_ref, x, next_ref, i, n_tiles, tm)
    t = i * tm + lax.broadcasted_iota(jnp.int32, (tm, 1), 0)
    ys = []
    for g, win in enumerate(POOL_WINDOWS):
        h = win // 2
        cols = slice(g * c, (g + 1) * c)
        wsum = buf_ref[HALO - h:HALO - h + tm, cols]
        for j in range(1, win):
            wsum = wsum + buf_ref[HALO - h + j:HALO - h + j + tm, cols]
        cnt = (jnp.minimum(t + h, seq) - jnp.maximum(t - h, 0)).astype(F32)
        pooled = wsum / cnt - x[:, cols]
        ys.append(jnp.dot(pooled.astype(BF16), w_ref[g], preferred_element_type=F32))
    y = jnp.concatenate(ys, axis=1) * scale_ref[...]
    out_ref[...] = _layer_norm(alpha * x + y, g_ref[...], b_ref[...])


def _pool_ln(x, w_bf16, scale, ln_g, ln_b, alpha):
    seq, d = x.shape
    tm = min(MIX_TILE, seq)
    n_tiles = seq // tm
    g, c, _ = w_bf16.shape
    row = lambda: pl.BlockSpec((1, d), lambda i: (0, 0))
    return pl.pallas_call(
        functools.partial(_pool_ln_kernel, tm=tm, n_tiles=n_tiles, seq=seq, alpha=alpha),
        grid=(n_tiles,),
        in_specs=_halo_specs(tm, seq, d) + [pl.BlockSpec((g, c, c), lambda i: (0, 0, 0)), row(), row(), row()],
        out_specs=pl.BlockSpec((tm, d), lambda i: (i, 0)),
        out_shape=jax.ShapeDtypeStruct((seq, d), F32),
        scratch_shapes=[pltpu.VMEM((tm + 2 * HALO, d), F32)],
        compiler_params=_params("arbitrary"),
        name="pool_ln",
    )(x, x, x, w_bf16, scale, ln_g, ln_b)


def _glu_kernel(x_ref, wa_ref, wg_ref, ba_ref, bg_ref, out_ref):
    x = x_ref[...].astype(BF16)
    a = jnp.dot(x, wa_ref[...], preferred_element_type=F32) + ba_ref[...]
    gate = jnp.dot(x, wg_ref[...], preferred_element_type=F32) + bg_ref[...]
    out_ref[...] = a * _sigmoid(gate)


def _glu(x, w1_bf16, b1):
    seq, d = x.shape
    tm = min(GLU_ROW_TILE, seq)
    tn = min(GLU_COL_TILE, d)
    nj = d // tn
    return pl.pallas_call(
        _glu_kernel,
        grid=(nj, seq // tm),
        in_specs=[
            pl.BlockSpec((tm, d), lambda j, i: (i, 0)),
            pl.BlockSpec((d, tn), lambda j, i: (0, j)),
            pl.BlockSpec((d, tn), lambda j, i: (0, nj + j)),
            pl.BlockSpec((1, tn), lambda j, i: (0, j)),
            pl.BlockSpec((1, tn), lambda j, i: (0, nj + j)),
        ],
        out_specs=pl.BlockSpec((tm, tn), lambda j, i: (i, j)),
        out_shape=jax.ShapeDtypeStruct((seq, d), F32),
        compiler_params=_params("arbitrary", "arbitrary"),
        name="conv_glu",
    )(x, w1_bf16, w1_bf16, b1, b1)


def _conv_ln_kernel(hprev_ref, hcur_ref, hnext_ref, x_ref, wdw_ref, bdw_ref, cg_ref, cb_ref, w2_ref, b2_ref,
                    mg_ref, mb_ref, out_ref, buf_ref, conv_ref, *, tm, n_tiles, alpha, width):
    i = pl.program_id(0)
    d = x_ref.shape[1]
    pad = width // 2
    _fill_halo_buffer(buf_ref, hprev_ref, hcur_ref[...], hnext_ref, i, n_tiles, tm)
    rows = min(64, tm)
    for cb in range(d // 128):
        cols = slice(cb * 128, (cb + 1) * 128)
        taps = [wdw_ref[j:j + 1, cols] for j in range(width)]
        bias = bdw_ref[:, cols]
        for rb in range(tm // rows):
            r0 = HALO - pad + rb * rows
            acc = bias + buf_ref[r0:r0 + rows, cols] * taps[0]
            for j in range(1, width):
                acc = acc + buf_ref[r0 + j:r0 + j + rows, cols] * taps[j]
            conv_ref[rb * rows:(rb + 1) * rows, cols] = acc
    hn = _layer_norm(conv_ref[...], cg_ref[...], cb_ref[...])
    hs = hn * _sigmoid(hn)
    y = jnp.dot(hs.astype(BF16), w2_ref[...], preferred_element_type=F32) + b2_ref[...]
    out_ref[...] = _layer_norm(alpha * x_ref[...] + y, mg_ref[...], mb_ref[...])


def _conv_ln(h, x, wdw, bdw, cg, cb, w2_bf16, b2, mg, mb, alpha):
    seq, d = x.shape
    tm = min(MIX_TILE, seq)
    n_tiles = seq // tm
    width = wdw.shape[0]
    row = lambda: pl.BlockSpec((1, d), lambda i: (0, 0))
    return pl.pallas_call(
        functools.partial(_conv_ln_kernel, tm=tm, n_tiles=n_tiles, alpha=alpha, width=width),
        grid=(n_tiles,),
        in_specs=_halo_specs(tm, seq, d) + [
            pl.BlockSpec((tm, d), lambda i: (i, 0)),
            pl.BlockSpec((width, d), lambda i: (0, 0)),
            row(), row(), row(),
            pl.BlockSpec((d, d), lambda i: (0, 0)),
            row(), row(), row(),
        ],
        out_specs=pl.BlockSpec((tm, d), lambda i: (i, 0)),
        out_shape=jax.ShapeDtypeStruct((seq, d), F32),
        scratch_shapes=[pltpu.VMEM((tm + 2 * HALO, d), F32), pltpu.VMEM((tm, d), F32)],
        compiler_params=_params("arbitrary"),
        name="conv_ln",
    )(h, h, h, x, wdw, bdw, cg, cb, w2_bf16, b2, mg, mb)


def _router_kernel(x_ref, wr_ref, br_ref, idx_ref, gate_ref, rank_ref, cnt_ref, carry_ref, *, tm, n_exp):
    i = pl.program_id(0)

    @pl.when(i == 0)
    def _():
        carry_ref[...] = jnp.zeros_like(carry_ref)

    logits = jnp.dot(x_ref[...], wr_ref[...], precision=lax.Precision.HIGHEST,
                     preferred_element_type=F32) + br_ref[...]
    lane = lax.broadcasted_iota(jnp.int32, (tm, n_exp), 1)
    work = logits
    vals, sels, hots = [], [], []
    for _ in range(TOP_K):
        m = jnp.max(work, axis=-1, keepdims=True)
        sel = jnp.min(jnp.where(work == m, lane, n_exp), axis=-1, keepdims=True)
        hot = lane == sel
        work = jnp.where(hot, -jnp.inf, work)
        vals.append(m)
        sels.append(sel)
        hots.append(hot)
    exps = [jnp.exp(v - vals[0]) for v in vals]
    denom = functools.reduce(lambda a, b: a + b, exps)
    multi = functools.reduce(lambda a, b: a + b, [h.astype(F32) for h in hots])
    strict_lower = (lax.broadcasted_iota(jnp.int32, (tm, tm), 0) > lax.broadcasted_iota(jnp.int32, (tm, tm), 1))
    cum = jnp.dot(strict_lower.astype(BF16), multi.astype(BF16), preferred_element_type=F32) + carry_ref[...]
    slot = lax.broadcasted_iota(jnp.int32, (tm, TOP_K), 1)
    idx = jnp.zeros((tm, TOP_K), jnp.int32)
    gate = jnp.zeros((tm, TOP_K), F32)
    rank = jnp.zeros((tm, TOP_K), jnp.int32)
    for k in range(TOP_K):
        r = jnp.sum(jnp.where(hots[k], cum, 0.0), axis=-1, keepdims=True)
        idx = jnp.where(slot == k, sels[k], idx)
        gate = jnp.where(slot == k, exps[k] / denom, gate)
        rank = jnp.where(slot == k, r.astype(jnp.int32), rank)
    idx_ref[...] = idx
    gate_ref[...] = gate
    rank_ref[...] = rank
    carry_ref[...] += jnp.sum(multi, axis=0, keepdims=True)
    cnt_ref[...] = carry_ref[...].astype(jnp.int32)


def _router(x, w_r, b_r):
    seq, d = x.shape
    n_exp = w_r.shape[1]
    tm = min(MIX_TILE, seq)
    per_tok = lambda dt: jax.ShapeDtypeStruct((seq, TOP_K), dt)
    tok_spec = lambda: pl.BlockSpec((tm, TOP_K), lambda i: (i, 0))
    return pl.pallas_call(
        functools.partial(_router_kernel, tm=tm, n_exp=n_exp),
        grid=(seq // tm,),
        in_specs=[
            pl.BlockSpec((tm, d), lambda i: (i, 0)),
            pl.BlockSpec((d, n_exp), lambda i: (0, 0)),
            pl.BlockSpec((1, n_exp), lambda i: (0, 0)),
        ],
        out_specs=[tok_spec(), tok_spec(), tok_spec(), pl.BlockSpec((1, n_exp), lambda i: (0, 0))],
        out_shape=[per_tok(jnp.int32), per_tok(F32), per_tok(jnp.int32), jax.ShapeDtypeStruct((1, n_exp), jnp.int32)],
        scratch_shapes=[pltpu.VMEM((1, n_exp), F32)],
        compiler_params=_params("arbitrary"),
        name="router",
    )(x, w_r, b_r)


def _moe_kernel(te_ref, ws_ref, clo_ref, chi_ref, nused_ref, uexp_ref, src_ref, dst_ref,
                x_hbm, w1_hbm, w2_hbm, b1_ref, b2_ref, y_hbm,
                xa, xb, ya, yb, wres, stage, gsem, ssem, wsem, *, tm, ch, n_real, d_ff, n_chunk_total):
    i = pl.program_id(0)
    nused = nused_ref[0]
    d = xa.shape[1]
    n_w1_chunks = d // ch
    chunks_per_expert = n_w1_chunks + d_ff // ch

    def weight_chunk_copy(q, start):
        e = uexp_ref[q // chunks_per_expert]
        c = lax.rem(q, chunks_per_expert)
        s = lax.rem(q, 2)

        @pl.when(c < n_w1_chunks)
        def _():
            cp = pltpu.make_async_copy(w1_hbm.at[e, pl.ds(pl.multiple_of(c * ch, ch), ch)], stage.at[s], wsem.at[s])
            cp.start() if start else cp.wait()

        @pl.when(c >= n_w1_chunks)
        def _():
            cp = pltpu.make_async_copy(w2_hbm.at[e, pl.ds(pl.multiple_of((c - n_w1_chunks) * ch, ch), ch)],
                                       stage.at[s], wsem.at[s])
            cp.start() if start else cp.wait()

    def consume_chunk(q, carry):
        weight_chunk_copy(q, start=False)
        slot = lax.rem(q // chunks_per_expert, 2)
        row0 = pl.multiple_of(lax.rem(q, chunks_per_expert) * ch, ch)
        wres[slot, pl.ds(row0, ch), :] = stage[lax.rem(q, 2)].astype(BF16)

        @pl.when(q + 2 < n_chunk_total[0])
        def _():
            weight_chunk_copy(q + 2, start=True)
        return carry

    @pl.when(i == 0)
    def _():
        weight_chunk_copy(0, start=True)

        @pl.when(n_chunk_total[0] > 1)
        def _():
            weight_chunk_copy(1, start=True)
        lax.fori_loop(0, chunks_per_expert, consume_chunk, 0)
        ya[...] = jnp.zeros_like(ya)
        yb[...] = jnp.zeros_like(yb)

        def first_gather(j, carry):
            pltpu.make_async_copy(x_hbm.at[pl.ds(src_ref[j], 1)], xa.at[pl.ds(j, 1)], gsem.at[0]).start()
            return carry
        lax.fori_loop(0, tm, first_gather, 0)

    lax.fori_loop(clo_ref[i], chi_ref[i], consume_chunk, 0)

    def gather_wait(buf, sem):
        pltpu.make_async_copy(x_hbm.at[pl.ds(0, tm)], buf, sem).wait()

    def scatter_wait(buf, sem):
        pltpu.make_async_copy(buf, y_hbm.at[pl.ds(0, tm)], sem).wait()

    def body(xcur, xnext, ycur, yprev, p):
        gather_wait(xcur, gsem.at[p])

        @pl.when(i >= 1)
        def _():
            scatter_wait(ycur, ssem.at[p])

        ws = ws_ref[i]
        x = xcur[...].astype(BF16)
        h = jnp.dot(x, wres[ws, 0:d, :], preferred_element_type=F32) + b1_ref[0]
        for j in range(tm):
            pltpu.make_async_copy(yprev.at[pl.ds(j, 1)], y_hbm.at[pl.ds(dst_ref[i * tm + j], 1)],
                                  ssem.at[1 - p]).start()
        for j in range(tm):
            pltpu.make_async_copy(x_hbm.at[pl.ds(src_ref[(i + 1) * tm + j], 1)], xnext.at[pl.ds(j, 1)],
                                  gsem.at[1 - p]).start()
        glu = jnp.minimum(h[:, :d_ff], SWIGLU_LIMIT)
        lin = jnp.clip(h[:, d_ff:], -SWIGLU_LIMIT, SWIGLU_LIMIT)
        act = glu * _sigmoid(SWIGLU_ALPHA * glu) * (lin + 1.0)
        ycur[...] = jnp.dot(act.astype(BF16), wres[ws, d:d + d_ff, :], preferred_element_type=F32) + b2_ref[0]

        @pl.when(i == nused - 1)
        def _():
            def last_scatter(j, carry):
                pltpu.make_async_copy(ycur.at[pl.ds(j, 1)], y_hbm.at[pl.ds(dst_ref[(i + 1) * tm + j], 1)],
                                      ssem.at[p]).start()
                return carry
            lax.fori_loop(0, tm, last_scatter, 0)
            scatter_wait(yprev, ssem.at[1 - p])
            scatter_wait(ycur, ssem.at[p])
            gather_wait(xnext, gsem.at[1 - p])

    @pl.when(jnp.logical_and(i < nused, i % 2 == 0))
    def _():
        body(xa, xb, ya, yb, 0)

    @pl.when(jnp.logical_and(i < nused, i % 2 == 1))
    def _():
        body(xb, xa, yb, ya, 1)


def _moe(x, plan, w1, b1, w2, b2, tm):
    seq, d = x.shape
    n_exp, _, two_f = w1.shape
    d_ff = two_f // 2
    assert two_f == d, "expert weights are staged through one (rows, d_model)-wide VMEM buffer"
    ch = min(256, d_ff)
    n_real = seq * TOP_K
    n_tiles = plan["tile_expert"].shape[0]
    smem_args = [plan[k] for k in ("tile_expert", "tile_wslot", "chunk_lo", "chunk_hi", "n_used", "used_experts",
                                   "src", "dst")]
    n_pref = len(smem_args) + 1
    idx = lambda i, te, *_: (te[i], 0, 0)
    grid_spec = pltpu.PrefetchScalarGridSpec(
        num_scalar_prefetch=n_pref,
        grid=(n_tiles,),
        in_specs=[
            pl.BlockSpec(memory_space=pl.ANY),
            pl.BlockSpec(memory_space=pl.ANY),
            pl.BlockSpec(memory_space=pl.ANY),
            pl.BlockSpec((1, 1, two_f), idx),
            pl.BlockSpec((1, 1, d), idx),
        ],
        out_specs=pl.BlockSpec(memory_space=pl.ANY),
        scratch_shapes=[
            pltpu.VMEM((tm, d), F32), pltpu.VMEM((tm, d), F32),
            pltpu.VMEM((tm, d), F32), pltpu.VMEM((tm, d), F32),
            pltpu.VMEM((2, d + d_ff, d), BF16),
            pltpu.VMEM((2, ch, d), F32),
            pltpu.SemaphoreType.DMA((2,)), pltpu.SemaphoreType.DMA((2,)), pltpu.SemaphoreType.DMA((2,)),
        ],
    )

    def kern(te, ws, clo, chi, nu, ue, src, dst, nct, *rest):
        _moe_kernel(te, ws, clo, chi, nu, ue, src, dst, *rest, tm=tm, ch=ch, n_real=n_real, d_ff=d_ff,
                    n_chunk_total=nct)

    return pl.pallas_call(
        kern,
        grid_spec=grid_spec,
        out_shape=jax.ShapeDtypeStruct((n_real + tm, d), F32),
        compiler_params=_params("arbitrary"),
        name="moe_ffn",
    )(*smem_args, plan["n_chunk_total"], x, w1, w2, b1.reshape(n_exp, 1, two_f), b2.reshape(n_exp, 1, d))


def _route_plan(idx, rank, counts, tm, n_pad, chunks_per_expert):
    seq, k = idx.shape
    n_exp = counts.shape[0]
    n_tiles = n_pad // tm
    i32 = jnp.int32
    tiles_per = (counts + tm - 1) // tm
    padded = tiles_per * tm
    gend = jnp.cumsum(padded)
    gstart = gend - padded
    pos = gstart[idx] + rank
    out_row = jnp.arange(k, dtype=i32)[None, :] * seq + jnp.arange(seq, dtype=i32)[:, None]
    flat_tok = jnp.broadcast_to(jnp.arange(seq, dtype=i32)[:, None], (seq, k))
    dump = seq * k + jnp.arange(tm, dtype=i32)
    dst = jnp.full((n_pad,), -1, i32).at[pos.reshape(-1)].set(out_row.reshape(-1))
    dst = jnp.where(dst < 0, jnp.tile(dump, n_tiles), dst)
    dst = jnp.concatenate([dump, dst])
    src = jnp.zeros((n_pad + tm,), i32).at[pos.reshape(-1)].set(flat_tok.reshape(-1))
    n_used = (gend[-1] // tm).astype(i32)
    tile = jnp.arange(n_tiles, dtype=i32)
    tile_c = jnp.minimum(tile, n_used - 1)
    tile_expert = jnp.minimum(jnp.sum(tile_c[:, None] * tm >= gend[None, :], axis=1), n_exp - 1).astype(i32)
    used = counts > 0
    ordinal = jnp.cumsum(used.astype(i32)) - 1
    n_used_exp = jnp.sum(used.astype(i32))
    used_experts = jnp.zeros((n_exp,), i32).at[jnp.where(used, ordinal, n_exp)].set(
        jnp.arange(n_exp, dtype=i32), mode="drop")
    r = ordinal[tile_expert]
    j = tile_c - gstart[tile_expert] // tm
    m = jnp.maximum(tiles_per[tile_expert], 1)
    n_chunk_total = n_used_exp * chunks_per_expert
    base = (r + 1) * chunks_per_expert
    lo = jnp.minimum(base + (chunks_per_expert * j) // m, n_chunk_total)
    hi = jnp.minimum(base + (chunks_per_expert * (j + 1)) // m, n_chunk_total)
    live = tile < n_used
    return {
        "tile_expert": tile_expert,
        "tile_wslot": (r % 2).astype(i32),
        "chunk_lo": jnp.where(live, lo, 0).astype(i32),
        "chunk_hi": jnp.where(live, hi, 0).astype(i32),
        "n_used": n_used.reshape(1),
        "used_experts": used_experts,
        "src": src,
        "dst": dst,
        "n_chunk_total": n_chunk_total.astype(i32).reshape(1),
    }


def _combine_ln_kernel(*refs, alpha):
    y_refs = refs[:TOP_K]
    gate_ref, x_ref, g_ref, b_ref, out_ref = refs[TOP_K:]
    gate = gate_ref[...]
    ffn = y_refs[0][...] * gate[:, 0:1]
    for k in range(1, TOP_K):
        ffn = ffn + y_refs[k][...] * gate[:, k:k + 1]
    out_ref[...] = _layer_norm(alpha * x_ref[...] + ffn, g_ref[...], b_ref[...])


def _combine_ln(y, gates, x, ln_g, ln_b, alpha):
    seq, d = x.shape
    tm = min(MIX_TILE, seq)
    n_tiles = seq // tm
    row = lambda: pl.BlockSpec((1, d), lambda i: (0, 0))
    y_specs = [pl.BlockSpec((tm, d), functools.partial(lambda i, k: (k * n_tiles + i, 0), k=k)) for k in range(TOP_K)]
    return pl.pallas_call(
        functools.partial(_combine_ln_kernel, alpha=alpha),
        grid=(n_tiles,),
        in_specs=y_specs + [pl.BlockSpec((tm, TOP_K), lambda i: (i, 0)), pl.BlockSpec((tm, d), lambda i: (i, 0)),
                            row(), row()],
        out_specs=pl.BlockSpec((tm, d), lambda i: (i, 0)),
        out_shape=jax.ShapeDtypeStruct((seq, d), F32),
        compiler_params=_params("arbitrary"),
        name="combine_ln",
    )(*([y] * TOP_K), gates, x, ln_g, ln_b)


def kernel(x, pool_w, pool_scale, conv_w1, conv_b1, conv_wdw, conv_bdw, conv_ln_g, conv_ln_b, conv_w2, conv_b2,
           mix_ln_g, mix_ln_b, router_w, router_b, moe_w1, moe_b1, moe_w2, moe_b2, ffn_ln_g, ffn_ln_b):
    batch, seq, d = x.shape
    depth = mix_ln_g.shape[0]
    n_exp = router_w.shape[-1]
    alpha = (2.0 * depth) ** 0.25
    tm_moe = min(MOE_TILE, seq)
    n_pad = seq * TOP_K + n_exp * tm_moe
    d_ff = moe_w2.shape[2]
    chunks_per_expert = (d + d_ff) // min(256, d_ff)
    row = lambda v: v.reshape(1, -1)
    outs = []
    for b in range(batch):
        xb = x[b]
        for i in range(depth):
            j = i // 2
            if i % 2 == 0:
                xb = _pool_ln(xb, pool_w[j].astype(BF16), row(pool_scale[j]), row(mix_ln_g[i]), row(mix_ln_b[i]), alpha)
            else:
                h = _glu(xb, conv_w1[j].astype(BF16), row(conv_b1[j]))
                xb = _conv_ln(h, xb, conv_wdw[j], row(conv_bdw[j]), row(conv_ln_g[j]), row(conv_ln_b[j]),
                              conv_w2[j].astype(BF16), row(conv_b2[j]), row(mix_ln_g[i]), row(mix_ln_b[i]), alpha)
            idx, gates, rank, counts = _router(xb, router_w[i], row(router_b[i]))
            plan = _route_plan(idx, rank, counts[0], tm_moe, n_pad, chunks_per_expert)
            y = _moe(xb, plan, moe_w1[i], moe_b1[i], moe_w2[i], moe_b2[i], tm_moe)
            xb = _combine_ln(y, gates, xb, row(ffn_ln_g[i]), row(ffn_ln_b[i]), alpha)
        outs.append(xb)
    return jnp.stack(outs, axis=0)
```

```python
import functools

import jax
import jax.numpy as jnp
from jax import lax
from jax.experimental import pallas as pl
from jax.experimental.pallas import tpu as pltpu

POOL_WINDOWS = (2, 4, 8, 16)
TOP_K = 4
SWIGLU_LIMIT = 7.0
SWIGLU_ALPHA = 1.702
LN_EPS = 1e-5

HALO = 16
MIX_TILE = 256
ROW_DMA_TILE = 128
GLU_ROW_TILE = 512
GLU_COL_TILE = 512
MOE_TILE = 256
WEIGHT_CHUNK_ROWS = 256
VMEM_LIMIT_BYTES = 56 * 1024 * 1024

F32 = jnp.float32
BF16 = jnp.bfloat16


def _layer_norm(z, g, b):
    mu = jnp.mean(z, axis=-1, keepdims=True)
    zc = z - mu
    var = jnp.mean(zc * zc, axis=-1, keepdims=True)
    return zc * lax.rsqrt(var + LN_EPS) * g + b


def _sigmoid(v):
    return 1.0 / (1.0 + jnp.exp(-v))


def _params(*sem):
    return pltpu.CompilerParams(dimension_semantics=sem, vmem_limit_bytes=VMEM_LIMIT_BYTES)


def _halo_specs(tm, seq, d):
    r = tm // HALO
    last = seq // HALO - 1
    return [
        pl.BlockSpec((HALO, d), lambda i: (jnp.maximum(i * r - 1, 0), 0)),
        pl.BlockSpec((tm, d), lambda i: (i, 0)),
        pl.BlockSpec((HALO, d), lambda i: (jnp.minimum((i + 1) * r, last), 0)),
    ]


def _fill_halo_buffer(buf_ref, prev_ref, cur, next_ref, i, n_tiles, tm):
    buf_ref[0:HALO, :] = jnp.where(i > 0, prev_ref[...], 0.0)
    buf_ref[HALO:HALO + tm, :] = cur
    buf_ref[HALO + tm:HALO + tm + HALO, :] = jnp.where(i < n_tiles - 1, next_ref[...], 0.0)


def _pool_ln_kernel(prev_ref, cur_ref, next_ref, w_ref, scale_ref, g_ref, b_ref, out_ref, buf_ref,
                    *, tm, n_tiles, seq, alpha):
    i = pl.program_id(0)
    d = cur_ref.shape[1]
    c = d // len(POOL_WINDOWS)
    x = cur_ref[...]
    _fill_halo_buffer(buf_ref, prev_ref, x, next_ref, i, n_tiles, tm)
    t = i * tm + lax.broadcasted_iota(jnp.int32, (tm, 1), 0)
    ys = []
    for g, win in enumerate(POOL_WINDOWS):
        h = win // 2
        cols = slice(g * c, (g + 1) * c)
        wsum = buf_ref[HALO - h:HALO - h + tm, cols]
        for j in range(1, win):
            wsum = wsum + buf_ref[HALO - h + j:HALO - h + j + tm, cols]
        cnt = (jnp.minimum(t + h, seq) - jnp.maximum(t - h, 0)).astype(F32)
        pooled = wsum / cnt - x[:, cols]
        ys.append(jnp.dot(pooled.astype(BF16), w_ref[g], preferred_element_type=F32))
    y = jnp.concatenate(ys, axis=1) * scale_ref[...]
    out_ref[...] = _layer_norm(alpha * x + y, g_ref[...], b_ref[...])


def _pool_ln(x, w_bf16, scale, ln_g, ln_b, alpha):
    seq, d = x.shape
    tm = min(MIX_TILE, seq)
    n_tiles = seq // tm
    g, c, _ = w_bf16.shape
    row = lambda: pl.BlockSpec((1, d), lambda i: (0, 0))
    return pl.pallas_call(
        functools.partial(_pool_ln_kernel, tm=tm, n_tiles=n_tiles, seq=seq, alpha=alpha),
        grid=(n_tiles,),
        in_specs=_halo_specs(tm, seq, d) + [pl.BlockSpec((g, c, c), lambda i: (0, 0, 0)), row(), row(), row()],
        out_specs=pl.BlockSpec((tm, d), lambda i: (i, 0)),
        out_shape=jax.ShapeDtypeStruct((seq, d), F32),
        scratch_shapes=[pltpu.VMEM((tm + 2 * HALO, d), F32)],
        compiler_params=_params("arbitrary"),
        name="pool_ln",
    )(x, x, x, w_bf16, scale, ln_g, ln_b)


def _glu_kernel(x_ref, wa_ref, wg_ref, ba_ref, bg_ref, out_ref):
    x = x_ref[...].astype(BF16)
    a = jnp.dot(x, wa_ref[...], preferred_element_type=F32) + ba_ref[...]
    gate = jnp.dot(x, wg_ref[...], preferred_element_type=F32) + bg_ref[...]
    out_ref[...] = a * _sigmoid(gate)


def _glu(x, w1_bf16, b1):
    seq, d = x.shape
    tm = min(GLU_ROW_TILE, seq)
    tn = min(GLU_COL_TILE, d)
    nj = d // tn
    return pl.pallas_call(
        _glu_kernel,
        grid=(nj, seq // tm),
        in_specs=[
            pl.BlockSpec((tm, d), lambda j, i: (i, 0)),
            pl.BlockSpec((d, tn), lambda j, i: (0, j)),
            pl.BlockSpec((d, tn), lambda j, i: (0, nj + j)),
            pl.BlockSpec((1, tn), lambda j, i: (0, j)),
            pl.BlockSpec((1, tn), lambda j, i: (0, nj + j)),
        ],
        out_specs=pl.BlockSpec((tm, tn), lambda j, i: (i, j)),
        out_shape=jax.ShapeDtypeStruct((seq, d), F32),
        compiler_params=_params("arbitrary", "arbitrary"),
        name="conv_glu",
    )(x, w1_bf16, w1_bf16, b1, b1)


def _conv_ln_kernel(hprev_ref, hcur_ref, hnext_ref, x_ref, wdw_ref, bdw_ref, cg_ref, cb_ref, w2_ref, b2_ref,
                    mg_ref, mb_ref, out_ref, buf_ref, conv_ref, *, tm, n_tiles, alpha, width):
    i = pl.program_id(0)
    d = x_ref.shape[1]
    pad = width // 2
    _fill_halo_buffer(buf_ref, hprev_ref, hcur_ref[...], hnext_ref, i, n_tiles, tm)
    rows = min(64, tm)
    for cb in range(d // 128):
        cols = slice(cb * 128, (cb + 1) * 128)
        taps = [wdw_ref[j:j + 1, cols] for j in range(width)]
        bias = bdw_ref[:, cols]
        for rb in range(tm // rows):
            r0 = HALO - pad + rb * rows
            acc = bias + buf_ref[r0:r0 + rows, cols] * taps[0]
            for j in range(1, width):
                acc = acc + buf_ref[r0 + j:r0 + j + rows, cols] * taps[j]
            conv_ref[rb * rows:(rb + 1) * rows, cols] = acc
    hn = _layer_norm(conv_ref[...], cg_ref[...], cb_ref[...])
    hs = hn * _sigmoid(hn)
    y = jnp.dot(hs.astype(BF16), w2_ref[...], preferred_element_type=F32) + b2_ref[...]
    out_ref[...] = _layer_norm(alpha * x_ref[...] + y, mg_ref[...], mb_ref[...])


def _conv_ln(h, x, wdw, bdw, cg, cb, w2_bf16, b2, mg, mb, alpha):
    seq, d = x.shape
    tm = min(MIX_TILE, seq)
    n_tiles = seq // tm
    width = wdw.shape[0]
    row = lambda: pl.BlockSpec((1, d), lambda i: (0, 0))
    return pl.pallas_call(
        functools.partial(_conv_ln_kernel, tm=tm, n_tiles=n_tiles, alpha=alpha, width=width),
        grid=(n_tiles,),
        in_specs=_halo_specs(tm, seq, d) + [
            pl.BlockSpec((tm, d), lambda i: (i, 0)),
            pl.BlockSpec((width, d), lambda i: (0, 0)),
            row(), row(), row(),
            pl.BlockSpec((d, d), lambda i: (0, 0)),
            row(), row(), row(),
        ],
        out_specs=pl.BlockSpec((tm, d), lambda i: (i, 0)),
        out_shape=jax.ShapeDtypeStruct((seq, d), F32),
        scratch_shapes=[pltpu.VMEM((tm + 2 * HALO, d), F32), pltpu.VMEM((tm, d), F32)],
        compiler_params=_params("arbitrary"),
        name="conv_ln",
    )(h, h, h, x, wdw, bdw, cg, cb, w2_bf16, b2, mg, mb)


def _router_kernel(x_ref, wr_ref, br_ref, idx_ref, gate_ref, rank_ref, cnt_ref, carry_ref, *, tm, n_exp):
    i = pl.program_id(0)

    @pl.when(i == 0)
    def _():
        carry_ref[...] = jnp.zeros_like(carry_ref)

    logits = jnp.dot(x_ref[...], wr_ref[...], precision=lax.Precision.HIGHEST,
                     preferred_element_type=F32) + br_ref[...]
    lane = lax.broadcasted_iota(jnp.int32, (tm, n_exp), 1)
    work = logits
    vals, sels, hots = [], [], []
    for _ in range(TOP_K):
        m = jnp.max(work, axis=-1, keepdims=True)
        sel = jnp.min(jnp.where(work == m, lane, n_exp), axis=-1, keepdims=True)
        hot = lane == sel
        work = jnp.where(hot, -jnp.inf, work)
        vals.append(m)
        sels.append(sel)
        hots.append(hot)
    exps = [jnp.exp(v - vals[0]) for v in vals]
    denom = functools.reduce(lambda a, b: a + b, exps)
    multi = functools.reduce(lambda a, b: a + b, [h.astype(F32) for h in hots])
    strict_lower = (lax.broadcasted_iota(jnp.int32, (tm, tm), 0) > lax.broadcasted_iota(jnp.int32, (tm, tm), 1))
    cum = jnp.dot(strict_lower.astype(BF16), multi.astype(BF16), preferred_element_type=F32) + carry_ref[...]
    slot = lax.broadcasted_iota(jnp.int32, (tm, TOP_K), 1)
    idx = jnp.zeros((tm, TOP_K), jnp.int32)
    gate = jnp.zeros((tm, TOP_K), F32)
    rank = jnp.zeros((tm, TOP_K), jnp.int32)
    for k in range(TOP_K):
        r = jnp.sum(jnp.where(hots[k], cum, 0.0), axis=-1, keepdims=True)
        idx = jnp.where(slot == k, sels[k], idx)
        gate = jnp.where(slot == k, exps[k] / denom, gate)
        rank = jnp.where(slot == k, r.astype(jnp.int32), rank)
    idx_ref[...] = idx
    gate_ref[...] = gate
    rank_ref[...] = rank
    carry_ref[...] += jnp.sum(multi, axis=0, keepdims=True)
    cnt_ref[...] = carry_ref[...].astype(jnp.int32)


def _router(x, w_r, b_r):
    seq, d = x.shape
    n_exp = w_r.shape[1]
    tm = min(MIX_TILE, seq)
    per_tok = lambda dt: jax.ShapeDtypeStruct((seq, TOP_K), dt)
    tok_spec = lambda: pl.BlockSpec((tm, TOP_K), lambda i: (i, 0))
    return pl.pallas_call(
        functools.partial(_router_kernel, tm=tm, n_exp=n_exp),
        grid=(seq // tm,),
        in_specs=[
            pl.BlockSpec((tm, d), lambda i: (i, 0)),
            pl.BlockSpec((d, n_exp), lambda i: (0, 0)),
            pl.BlockSpec((1, n_exp), lambda i: (0, 0)),
        ],
        out_specs=[tok_spec(), tok_spec(), tok_spec(), pl.BlockSpec((1, n_exp), lambda i: (0, 0))],
        out_shape=[per_tok(jnp.int32), per_tok(F32), per_tok(jnp.int32), jax.ShapeDtypeStruct((1, n_exp), jnp.int32)],
        scratch_shapes=[pltpu.VMEM((1, n_exp), F32)],
        compiler_params=_params("arbitrary"),
        name="router",
    )(x, w_r, b_r)


def _route_plan(idx, rank, counts, tm, n_rows, chunks_per_expert):
    n_exp = counts.shape[0]
    n_items_max = n_rows // tm + n_exp - 1
    i32 = jnp.int32
    gend = jnp.cumsum(counts)
    gstart = gend - counts
    pos = (gstart[idx] + rank).astype(i32)
    used = counts > 0
    first_tile = gstart // tm
    n_it = jnp.where(used, (gend - 1) // tm - first_tile + 1, 0)
    it_end = jnp.cumsum(n_it)
    it_start = it_end - n_it
    n_items = it_end[-1]
    w = jnp.arange(n_items_max, dtype=i32)
    wc = jnp.minimum(w, n_items - 1)
    e_w = jnp.minimum(jnp.sum(wc[:, None] >= it_end[None, :], axis=1), n_exp - 1).astype(i32)
    j = wc - it_start[e_w]
    t_w = (first_tile[e_w] + j).astype(i32)
    first = jnp.concatenate([jnp.ones((1,), i32), (t_w[1:] != t_w[:-1]).astype(i32)])
    ordinal = jnp.cumsum(used.astype(i32)) - 1
    n_used_exp = jnp.sum(used.astype(i32))
    used_experts = jnp.zeros((n_exp,), i32).at[jnp.where(used, ordinal, n_exp)].set(
        jnp.arange(n_exp, dtype=i32), mode="drop")
    r = ordinal[e_w]
    m = jnp.maximum(n_it[e_w], 1)
    n_chunk_total = n_used_exp * chunks_per_expert
    base = (r + 1) * chunks_per_expert
    lo = jnp.minimum(base + (chunks_per_expert * j) // m, n_chunk_total)
    hi = jnp.minimum(base + (chunks_per_expert * (j + 1)) // m, n_chunk_total)
    live = w < n_items
    plan = {
        "item_tile": t_w,
        "item_expert": e_w,
        "item_first": first,
        "row_lo": gstart[e_w].astype(i32),
        "row_hi": gend[e_w].astype(i32),
        "item_wslot": (r % 2).astype(i32),
        "chunk_lo": jnp.where(live, lo, 0).astype(i32),
        "chunk_hi": jnp.where(live, hi, 0).astype(i32),
        "n_items": n_items.astype(i32).reshape(1),
        "used_experts": used_experts,
        "n_chunk_total": n_chunk_total.astype(i32).reshape(1),
    }
    return pos, plan


def _dispatch_kernel(pos_ref, x_ref, xs_hbm, stage, sem, *, tm, n_tiles):
    i = pl.program_id(0)

    def wait_tile(slot):
        for _ in range(TOP_K):
            pltpu.make_async_copy(stage.at[slot], xs_hbm.at[pl.ds(0, tm)], sem.at[slot]).wait()

    def send(slot):
        @pl.when(i >= 2)
        def _():
            wait_tile(slot)

        stage[slot] = x_ref[...]
        for r in range(tm):
            for k in range(TOP_K):
                p = pos_ref[(i * tm + r) * TOP_K + k]
                pltpu.make_async_copy(stage.at[slot, pl.ds(r, 1)], xs_hbm.at[pl.ds(p, 1)], sem.at[slot]).start()

    @pl.when(jnp.logical_and(i < n_tiles, i % 2 == 0))
    def _():
        send(0)

    @pl.when(jnp.logical_and(i < n_tiles, i % 2 == 1))
    def _():
        send(1)

    @pl.when(i == n_tiles)
    def _():
        if n_tiles >= 2:
            wait_tile(n_tiles % 2)
        wait_tile((n_tiles - 1) % 2)


def _dispatch(x, pos_flat):
    seq, d = x.shape
    tm = min(ROW_DMA_TILE, seq)
    n_tiles = seq // tm
    grid_spec = pltpu.PrefetchScalarGridSpec(
        num_scalar_prefetch=1,
        grid=(n_tiles + 1,),
        in_specs=[pl.BlockSpec((tm, d), lambda i, pos: (jnp.minimum(i, n_tiles - 1), 0))],
        out_specs=pl.BlockSpec(memory_space=pl.ANY),
        scratch_shapes=[pltpu.VMEM((2, tm, d), F32), pltpu.SemaphoreType.DMA((2,))],
    )
    return pl.pallas_call(
        functools.partial(_dispatch_kernel, tm=tm, n_tiles=n_tiles),
        grid_spec=grid_spec,
        out_shape=jax.ShapeDtypeStruct((seq * TOP_K, d), F32),
        compiler_params=_params("arbitrary"),
        name="dispatch",
    )(pos_flat, x)


def _moe_kernel(tile_ref, te_ref, first_ref, lo_ref, hi_ref, ws_ref, clo_ref, chi_ref, nitems_ref, uexp_ref, nct_ref,
                x_ref, w1_hbm, w2_hbm, b1_ref, b2_ref, y_ref, wres, stage, wsem, *, layer, tm, ch, d_ff):
    w = pl.program_id(0)
    d = x_ref.shape[1]
    n_w1_chunks = d // ch
    chunks_per_expert = n_w1_chunks + d_ff // ch

    def weight_chunk_copy(q, start):
        e = uexp_ref[q // chunks_per_expert]
        c = lax.rem(q, chunks_per_expert)
        s = lax.rem(q, 2)

        @pl.when(c < n_w1_chunks)
        def _():
            cp = pltpu.make_async_copy(w1_hbm.at[layer, e, pl.ds(pl.multiple_of(c * ch, ch), ch)], stage.at[s],
                                       wsem.at[s])
            cp.start() if start else cp.wait()

        @pl.when(c >= n_w1_chunks)
        def _():
            cp = pltpu.make_async_copy(w2_hbm.at[layer, e, pl.ds(pl.multiple_of((c - n_w1_chunks) * ch, ch), ch)],
                                       stage.at[s], wsem.at[s])
            cp.start() if start else cp.wait()

    def consume_chunk(q, carry):
        weight_chunk_copy(q, start=False)
        slot = lax.rem(q // chunks_per_expert, 2)
        row0 = pl.multiple_of(lax.rem(q, chunks_per_expert) * ch, ch)
        wres[slot, pl.ds(row0, ch), :] = stage[lax.rem(q, 2)].astype(BF16)

        @pl.when(q + 2 < nct_ref[0])
        def _():
            weight_chunk_copy(q + 2, start=True)
        return carry

    @pl.when(w == 0)
    def _():
        weight_chunk_copy(0, start=True)

        @pl.when(nct_ref[0] > 1)
        def _():
            weight_chunk_copy(1, start=True)
        lax.fori_loop(0, chunks_per_expert, consume_chunk, 0)

    lax.fori_loop(clo_ref[w], chi_ref[w], consume_chunk, 0)

    @pl.when(w < nitems_ref[0])
    def _():
        ws = ws_ref[w]
        x = x_ref[...].astype(BF16)
        h = jnp.dot(x, wres[ws, 0:d, :], preferred_element_type=F32) + b1_ref[0]
        glu = jnp.minimum(h[:, :d_ff], SWIGLU_LIMIT)
        lin = jnp.clip(h[:, d_ff:], -SWIGLU_LIMIT, SWIGLU_LIMIT)
        act = glu * _sigmoid(SWIGLU_ALPHA * glu) * (lin + 1.0)
        y = jnp.dot(act.astype(BF16), wres[ws, d:d + d_ff, :], preferred_element_type=F32) + b2_ref[0]
        row = tile_ref[w] * tm + lax.broadcasted_iota(jnp.int32, (tm, 1), 0)
        mine = jnp.logical_and(row >= lo_ref[w], row < hi_ref[w])
        y = jnp.where(mine, y, 0.0)

        @pl.when(first_ref[w] == 1)
        def _():
            y_ref[...] = y

        @pl.when(first_ref[w] == 0)
        def _():
            y_ref[...] += y


def _moe(xs, plan, w1, b1, w2, b2, layer, tm):
    n_rows, d = xs.shape
    _, n_exp, _, two_f = w1.shape
    d_ff = two_f // 2
    assert two_f == d, "expert weights are staged through one (rows, d_model)-wide VMEM buffer"
    ch = min(WEIGHT_CHUNK_ROWS, d_ff)
    smem_args = [plan[k] for k in ("item_tile", "item_expert", "item_first", "row_lo", "row_hi", "item_wslot",
                                   "chunk_lo", "chunk_hi", "n_items", "used_experts", "n_chunk_total")]
    n_items_max = plan["item_tile"].shape[0]
    by_tile = lambda w, tile, *_: (tile[w], 0)
    by_expert = lambda w, tile, te, *_: (layer * n_exp + te[w], 0, 0)
    grid_spec = pltpu.PrefetchScalarGridSpec(
        num_scalar_prefetch=len(smem_args),
        grid=(n_items_max,),
        in_specs=[
            pl.BlockSpec((tm, d), by_tile),
            pl.BlockSpec(memory_space=pl.ANY),
            pl.BlockSpec(memory_space=pl.ANY),
            pl.BlockSpec((1, 1, two_f), by_expert),
            pl.BlockSpec((1, 1, d), by_expert),
        ],
        out_specs=pl.BlockSpec((tm, d), by_tile),
        scratch_shapes=[
            pltpu.VMEM((2, d + d_ff, d), BF16),
            pltpu.VMEM((2, ch, d), F32),
            pltpu.SemaphoreType.DMA((2,)),
        ],
    )
    return pl.pallas_call(
        functools.partial(_moe_kernel, layer=layer, tm=tm, ch=ch, d_ff=d_ff),
        grid_spec=grid_spec,
        out_shape=jax.ShapeDtypeStruct((n_rows, d), F32),
        compiler_params=_params("arbitrary"),
        name="moe_ffn",
    )(*smem_args, xs, w1, w2, b1.reshape(-1, 1, two_f), b2.reshape(-1, 1, d))


def _combine_ln_kernel(pos_ref, gate_ref, x_ref, g_ref, b_ref, ys_hbm, out_ref, gbuf, sem, *, tm, n_tiles, alpha):
    i = pl.program_id(0)

    def issue(slot):
        for r in range(tm):
            for k in range(TOP_K):
                p = pos_ref[(i * tm + r) * TOP_K + k]
                pltpu.make_async_copy(ys_hbm.at[pl.ds(p, 1)], gbuf.at[slot, k, pl.ds(r, 1)], sem.at[slot]).start()

    def finish(slot):
        for k in range(TOP_K):
            pltpu.make_async_copy(ys_hbm.at[pl.ds(0, tm)], gbuf.at[slot, k], sem.at[slot]).wait()
        gate = gate_ref[...]
        ffn = gbuf[slot, 0] * gate[:, 0:1]
        for k in range(1, TOP_K):
            ffn = ffn + gbuf[slot, k] * gate[:, k:k + 1]
        out_ref[...] = _layer_norm(alpha * x_ref[...] + ffn, g_ref[...], b_ref[...])

    for parity in range(2):
        @pl.when(i % 2 == parity)
        def _():
            @pl.when(i < n_tiles)
            def _():
                issue(parity)

            @pl.when(i >= 1)
            def _():
                finish(1 - parity)


def _combine_ln(ys, pos_flat, gates, x, ln_g, ln_b, alpha):
    seq, d = x.shape
    tm = min(ROW_DMA_TILE, seq)
    n_tiles = seq // tm
    prev = lambda i, pos: (jnp.maximum(i - 1, 0), 0)
    row = lambda: pl.BlockSpec((1, d), lambda i, pos: (0, 0))
    grid_spec = pltpu.PrefetchScalarGridSpec(
        num_scalar_prefetch=1,
        grid=(n_tiles + 1,),
        in_specs=[pl.BlockSpec((tm, TOP_K), prev), pl.BlockSpec((tm, d), prev), row(), row(),
                  pl.BlockSpec(memory_space=pl.ANY)],
        out_specs=pl.BlockSpec((tm, d), prev),
        scratch_shapes=[pltpu.VMEM((2, TOP_K, tm, d), F32), pltpu.SemaphoreType.DMA((2,))],
    )
    return pl.pallas_call(
        functools.partial(_combine_ln_kernel, tm=tm, n_tiles=n_tiles, alpha=alpha),
        grid_spec=grid_spec,
        out_shape=jax.ShapeDtypeStruct((seq, d), F32),
        compiler_params=_params("arbitrary"),
        name="combine_ln",
    )(pos_flat, gates, x, ln_g, ln_b, ys)


def kernel(x, pool_w, pool_scale, conv_w1, conv_b1, conv_wdw, conv_bdw, conv_ln_g, conv_ln_b, conv_w2, conv_b2,
           mix_ln_g, mix_ln_b, router_w, router_b, moe_w1, moe_b1, moe_w2, moe_b2, ffn_ln_g, ffn_ln_b):
    batch, seq, d = x.shape
    depth = mix_ln_g.shape[0]
    alpha = (2.0 * depth) ** 0.25
    n_rows = seq * TOP_K
    tm_moe = min(MOE_TILE, n_rows)
    d_ff = moe_w2.shape[2]
    chunks_per_expert = (d + d_ff) // min(WEIGHT_CHUNK_ROWS, d_ff)
    row = lambda v: v.reshape(1, -1)
    outs = []
    for b in range(batch):
        xb = x[b]
        for i in range(depth):
            j = i // 2
            if i % 2 == 0:
                xb = _pool_ln(xb, pool_w[j].astype(BF16), row(pool_scale[j]), row(mix_ln_g[i]), row(mix_ln_b[i]), alpha)
            else:
                h = _glu(xb, conv_w1[j].astype(BF16), row(conv_b1[j]))
                xb = _conv_ln(h, xb, conv_wdw[j], row(conv_bdw[j]), row(conv_ln_g[j]), row(conv_ln_b[j]),
                              conv_w2[j].astype(BF16), row(conv_b2[j]), row(mix_ln_g[i]), row(mix_ln_b[i]), alpha)
            idx, gates, rank, counts = _router(xb, router_w[i], row(router_b[i]))
            pos, plan = _route_plan(idx, rank, counts[0], tm_moe, n_rows, chunks_per_expert)
            pos_flat = pos.reshape(-1)
            xs = _dispatch(xb, pos_flat)
            ys = _moe(xs, plan, moe_w1, moe_b1, moe_w2, moe_b2, i, tm_moe)
            xb = _combine_ln(ys, pos_flat, gates, xb, row(ffn_ln_g[i]), row(ffn_ln_b[i]), alpha)
        outs.append(xb)
    return jnp.stack(outs, axis=0)
```
